```python
import jax, jax.numpy as jnp
from jax import lax
import numpy as np

D_MODEL = 1024
BATCH = 32
SEQ = 256
DEPTH = 2
DEC_BATCH = 4
DEC_SEQ = 2048
PAST_LEN = 256

GRID_W = 64
BLOCK = 128
ROPE_THETA = 10000.0
EPS = 1e-6
N_ATT = (DEPTH + 1) // 2
N_SSD = DEPTH // 2
H_A = 8
KVH_A = 2
HD_A = 64
WINDOW = 128
H_B = 8
Q_RANK = 256
KV_RANK = 128
NOPE_B = 64
ROPE_B = 32
V_B = 64
QK_B = NOPE_B + ROPE_B
IN_ATT = H_A * HD_A + 2 * KVH_A * HD_A + Q_RANK + KV_RANK + ROPE_B
OUT_ATT = H_A * HD_A + H_B * V_B
D_INNER = 2 * D_MODEL
P_C = 64
H_C = D_INNER // P_C
G_C = 8
N_C = 128
D_CONV = 5
CHUNK = 128
CONV_CH = D_INNER + 2 * G_C * N_C
IN_SSD = D_INNER + CONV_CH + 2 * H_C
N_GROUPS_E = 4
EXPERTS_PER_GROUP = 8
N_EXPERTS = N_GROUPS_E * EXPERTS_PER_GROUP
TOP_K_E = 2
D_EXPERT = 256

kernel_name = "hybrid_diffusion_prefix_trunk_step"


def rmsnorm(x, g):
    xf = x.astype(jnp.float32)
    y = xf * lax.rsqrt(jnp.mean(xf * xf, axis=-1, keepdims=True) + EPS)
    return (y * g.astype(jnp.float32)).astype(x.dtype)


def adaln(cond, w, b):
    mod = jax.nn.silu(cond) @ w + b
    return jnp.split(mod[:, None, :], 6, axis=-1)


def modulate(x, g, shift, scale):
    return rmsnorm(x, g) * (1 + scale) + shift


def axial_rope(n_tokens, rot_dim):
    rows = n_tokens // GRID_W
    t = jnp.arange(rows * GRID_W)
    row = (t // GRID_W).astype(jnp.float32)
    col = (t % GRID_W).astype(jnp.float32)
    quarter = rot_dim // 4
    inv = ROPE_THETA ** (-jnp.arange(quarter, dtype=jnp.float32) / quarter)
    ar = row[:, None] * inv[None, :]
    ac = col[:, None] * inv[None, :]
    ang = jnp.concatenate([ar, ar, ac, ac], axis=-1)
    return jnp.cos(ang), jnp.sin(ang)


def _rot_half(u):
    u1, u2 = jnp.split(u, 2, axis=-1)
    return jnp.concatenate([-u2, u1], axis=-1)


def apply_axial_rope(x, cos, sin):
    half = x.shape[-1] // 2
    rotated = jnp.concatenate([_rot_half(x[..., :half]), _rot_half(x[..., half:])], axis=-1)
    c = cos[:, None, :].astype(x.dtype)
    s = sin[:, None, :].astype(x.dtype)
    return x * c + rotated * s


def ab_project(h, w_in, a_qnorm, a_knorm, b_qa_norm, b_wuq, b_kva_norm, b_qnorm):
    bsz, L, _ = h.shape
    proj = h @ w_in
    q_a, k_a, v_a, q_lat, ckv, krope = jnp.split(
        proj, [H_A * HD_A, H_A * HD_A + KVH_A * HD_A, H_A * HD_A + 2 * KVH_A * HD_A,
               H_A * HD_A + 2 * KVH_A * HD_A + Q_RANK,
               H_A * HD_A + 2 * KVH_A * HD_A + Q_RANK + KV_RANK], axis=-1)
    q_a = rmsnorm(q_a.reshape(bsz, L, H_A, HD_A), a_qnorm)
    k_a = rmsnorm(k_a.reshape(bsz, L, KVH_A, HD_A), a_knorm)
    v_a = v_a.reshape(bsz, L, KVH_A, HD_A)
    q_b = (rmsnorm(q_lat, b_qa_norm) @ b_wuq).reshape(bsz, L, H_B, QK_B)
    q_b = rmsnorm(q_b, b_qnorm)
    ckv = rmsnorm(ckv, b_kva_norm)
    return q_a, k_a, v_a, q_b, ckv, krope


def mla_keys(ckv, krope, b_wukv, b_knorm):
    bsz, L, _ = ckv.shape
    kv = (ckv @ b_wukv).reshape(bsz, L, H_B, NOPE_B + V_B)
    k_nope, v = kv[..., :NOPE_B], kv[..., NOPE_B:]
    k = jnp.concatenate([k_nope, jnp.broadcast_to(krope[:, :, None, :], (bsz, L, H_B, ROPE_B))], axis=-1)
    return rmsnorm(k, b_knorm), v


def ctx_sink_attention(q, k, v, sink):
    bsz, L, _, _ = q.shape
    grp = H_A // KVH_A
    qg = q.reshape(bsz, L, KVH_A, grp, HD_A)
    s = jnp.einsum('bqkgd,bskd->bkgqs', qg, k).astype(jnp.float32) * HD_A ** -0.5
    sink_col = jnp.broadcast_to(sink.astype(jnp.float32).reshape(1, KVH_A, grp, 1, 1), (bsz, KVH_A, grp, L, 1))
    p = jax.nn.softmax(jnp.concatenate([s, sink_col], axis=-1), axis=-1)[..., :-1].astype(v.dtype)
    o = jnp.einsum('bkgqs,bskd->bqkgd', p, v)
    return o.reshape(bsz, L, H_A * HD_A)


def window_sink_attention(q, k, v, k_ctx, v_ctx, sink):
    bsz, L, _, _ = q.shape
    grp = H_A // KVH_A
    nb = L // BLOCK
    lc = k_ctx.shape[1]
    qb = q.reshape(bsz, nb, BLOCK, KVH_A, grp, HD_A)
    pad = ((0, 0), (BLOCK, BLOCK), (0, 0), (0, 0))
    kp = jnp.pad(k, pad).reshape(bsz, nb + 2, BLOCK, KVH_A, HD_A)
    vp = jnp.pad(v, pad).reshape(bsz, nb + 2, BLOCK, KVH_A, HD_A)
    kband = jnp.concatenate([kp[:, :-2], kp[:, 1:-1], kp[:, 2:]], axis=2)
    vband = jnp.concatenate([vp[:, :-2], vp[:, 1:-1], vp[:, 2:]], axis=2)
    qi = jnp.arange(BLOCK)
    kj = jnp.arange(3 * BLOCK)
    blk = jnp.arange(nb)
    key_pos = blk[:, None] * BLOCK - BLOCK + kj[None, :]
    rel = kj[None, :] - BLOCK - qi[:, None]
    mask = (jnp.abs(rel) <= WINDOW)[None] & ((key_pos >= 0) & (key_pos < L))[:, None, :]
    scale = HD_A ** -0.5
    s_loc = jnp.einsum('bnqkgd,bnskd->bnkgqs', qb, kband).astype(jnp.float32) * scale
    s_loc = jnp.where(mask[None, :, None, None], s_loc, -jnp.inf)
    s_ctx = jnp.einsum('bnqkgd,bskd->bnkgqs', qb, k_ctx).astype(jnp.float32) * scale
    sink_col = jnp.broadcast_to(sink.astype(jnp.float32).reshape(1, 1, KVH_A, grp, 1, 1), (bsz, nb, KVH_A, grp, BLOCK, 1))
    p = jax.nn.softmax(jnp.concatenate([s_ctx, s_loc, sink_col], axis=-1), axis=-1).astype(v.dtype)
    p_ctx = p[..., :lc]
    p_loc = p[..., lc:lc + 3 * BLOCK]
    o = jnp.einsum('bnkgqs,bskd->bnqkgd', p_ctx, v_ctx) + jnp.einsum('bnkgqs,bnskd->bnqkgd', p_loc, vband)
    return o.reshape(bsz, L, H_A * HD_A)


def dense_attention(q, k, v):
    bsz, L, _, _ = q.shape
    s = jnp.einsum('bqhd,bshd->bhqs', q, k).astype(jnp.float32) * QK_B ** -0.5
    p = jax.nn.softmax(s, axis=-1).astype(v.dtype)
    return jnp.einsum('bhqs,bshe->bqhe', p, v).reshape(bsz, L, H_B * V_B)


def mla_latent_attention(q, k, v, k_ctx, v_ctx):
    bsz, L, _, _ = q.shape
    nb = L // BLOCK
    k_all = jnp.concatenate([k_ctx, k], axis=1)
    v_all = jnp.concatenate([v_ctx, v], axis=1)
    qb = jnp.moveaxis(q.reshape(bsz, nb, BLOCK, H_B, QK_B), 1, 0)

    def one_block(qblk):
        s = jnp.einsum('bqhd,bshd->bhqs', qblk, k_all).astype(jnp.float32) * QK_B ** -0.5
        p = jax.nn.softmax(s, axis=-1).astype(v_all.dtype)
        return jnp.einsum('bhqs,bshe->bqhe', p, v_all)

    o = lax.map(one_block, qb)
    return jnp.moveaxis(o, 0, 1).reshape(bsz, L, H_B * V_B)


def att_mixer_context(h, w_in, a_qnorm, a_knorm, a_sink, b_qa_norm, b_wuq, b_kva_norm, b_wukv, b_qnorm, b_knorm, w_out):
    q_a, k_a, v_a, q_b, ckv, krope = ab_project(h, w_in, a_qnorm, a_knorm, b_qa_norm, b_wuq, b_kva_norm, b_qnorm)
    o_a = ctx_sink_attention(q_a, k_a, v_a, a_sink)
    k_b, v_b = mla_keys(ckv, krope, b_wukv, b_knorm)
    o_b = dense_attention(q_b, k_b, v_b)
    out = jnp.concatenate([o_a, o_b], axis=-1) @ w_out
    return out, k_a, v_a, ckv, krope


def att_mixer_latent(h, ka_ctx, va_ctx, ckv_ctx, krope_ctx, w_in, a_qnorm, a_knorm, a_sink, b_qa_norm, b_wuq,
                     b_kva_norm, b_wukv, b_qnorm, b_knorm, w_out):
    L = h.shape[1]
    cos_a, sin_a = axial_rope(L, HD_A)
    cos_b, sin_b = axial_rope(L, ROPE_B)
    q_a, k_a, v_a, q_b, ckv, krope = ab_project(h, w_in, a_qnorm, a_knorm, b_qa_norm, b_wuq, b_kva_norm, b_qnorm)
    q_a = apply_axial_rope(q_a, cos_a, sin_a)
    k_a = apply_axial_rope(k_a, cos_a, sin_a)
    o_a = window_sink_attention(q_a, k_a, v_a, ka_ctx, va_ctx, a_sink)
    k_b, v_b = mla_keys(ckv, krope, b_wukv, b_knorm)
    k_bc, v_bc = mla_keys(ckv_ctx, krope_ctx, b_wukv, b_knorm)
    q_b = jnp.concatenate([q_b[..., :NOPE_B], apply_axial_rope(q_b[..., NOPE_B:], cos_b, sin_b)], axis=-1)
    k_b = jnp.concatenate([k_b[..., :NOPE_B], apply_axial_rope(k_b[..., NOPE_B:], cos_b, sin_b)], axis=-1)
    o_b = mla_latent_attention(q_b, k_b, v_b, k_bc, v_bc)
    return jnp.concatenate([o_a, o_b], axis=-1) @ w_out


def centred_dwconv(u, w, bias):
    ch = u.shape[-1]
    k = w.shape[0]
    out = lax.conv_general_dilated(u, w[:, None, :].astype(u.dtype), window_strides=(1,),
                                   padding=[(k // 2, k // 2)], dimension_numbers=('NWC', 'WIO', 'NWC'),
                                   feature_group_count=ch)
    return out + bias


def ssd_scan(x, dt, a_neg, bm, cm, s0):
    f32 = jnp.float32
    bsz, L, H, P = x.shape
    G, N = bm.shape[2], bm.shape[3]
    R = H // G
    nc = L // CHUNK
    xc = x.astype(f32).reshape(bsz, nc, CHUNK, G, R, P)
    dtc = dt.astype(f32).reshape(bsz, nc, CHUNK, G, R)
    bc = bm.astype(f32).reshape(bsz, nc, CHUNK, G, N)
    cc = cm.astype(f32).reshape(bsz, nc, CHUNK, G, N)
    acs = jnp.cumsum(dtc * a_neg.astype(f32).reshape(G, R), axis=2)
    tri = jnp.tril(jnp.ones((CHUNK, CHUNK), dtype=bool))
    seg = acs[:, :, :, None] - acs[:, :, None, :]
    decay = jnp.exp(jnp.where(tri[:, :, None, None], seg, -jnp.inf))
    cb = jnp.einsum('bcign,bcjgn->bcijg', cc, bc)
    y_intra = jnp.einsum('bcijgr,bcjgrp->bcigrp', cb[..., None] * decay * dtc[:, :, None], xc)
    to_end = jnp.exp(acs[:, :, -1:] - acs) * dtc
    chunk_states = jnp.einsum('bcjgn,bcjgrp->bcgrpn', bc, to_end[..., None] * xc)
    chunk_decay = jnp.exp(acs[:, :, -1])

    def step(s, inp):
        dec, st = inp
        return dec[..., None, None] * s + st, s

    s_fin, s_prev = lax.scan(step, s0.astype(f32).reshape(bsz, G, R, P, N),
                             (jnp.moveaxis(chunk_decay, 1, 0), jnp.moveaxis(chunk_states, 1, 0)))
    s_prev = jnp.moveaxis(s_prev, 0, 1)
    y_inter = jnp.einsum('bcign,bcgrpn->bcigrp', cc, s_prev) * jnp.exp(acs)[..., None]
    y = (y_intra + y_inter).reshape(bsz, L, H, P)
    return y, s_fin.reshape(bsz, H, P, N)


def ssd_mixer(h, s0_f, s0_b, w_in, conv_w, conv_b, alog_f, alog_b, dtb_f, dtb_b, d_skip, norm_w, w_out):
    f32 = jnp.float32
    bsz, L, _ = h.shape
    proj = h @ w_in
    z, xbc, dt_f, dt_b = jnp.split(proj, [D_INNER, D_INNER + CONV_CH, D_INNER + CONV_CH + H_C], axis=-1)
    xbc = jax.nn.silu(centred_dwconv(xbc, conv_w, conv_b))
    xs, bm, cm = jnp.split(xbc, [D_INNER, D_INNER + G_C * N_C], axis=-1)
    xs = xs.reshape(bsz, L, H_C, P_C)
    bm = bm.reshape(bsz, L, G_C, N_C)
    cm = cm.reshape(bsz, L, G_C, N_C)
    dtf = jax.nn.softplus((dt_f + dtb_f).astype(f32))
    dtbk = jax.nn.softplus((dt_b + dtb_b).astype(f32))
    y_f, s_f = ssd_scan(xs, dtf, -jnp.exp(alog_f.astype(f32)), bm, cm, s0_f)
    flip = lambda t: jnp.flip(t, axis=1)
    y_b, s_b = ssd_scan(flip(xs), flip(dtbk), -jnp.exp(alog_b.astype(f32)), flip(bm), flip(cm), s0_b)
    y = y_f + flip(y_b) + d_skip.astype(f32)[:, None] * xs.astype(f32)
    y = y.reshape(bsz, L, D_INNER) * jax.nn.silu(z.astype(f32))
    yg = y.reshape(bsz, L, G_C, D_INNER // G_C)
    yg = yg * lax.rsqrt(jnp.mean(yg * yg, axis=-1, keepdims=True) + EPS)
    y = yg.reshape(bsz, L, D_INNER) * norm_w.astype(f32)
    return y.astype(h.dtype) @ w_out, s_f.astype(h.dtype), s_b.astype(h.dtype)


def hier_moe(h, wg, bg, we, be, w1, w3, w2):
    bsz, L, d = h.shape
    t = h.reshape(-1, d)
    n = t.shape[0]
    p_grp = jax.nn.softmax((t @ wg + bg).astype(jnp.float32), axis=-1)
    p_top, g_idx = lax.top_k(p_grp, 1)
    e_logits = (t @ we + be).astype(jnp.float32).reshape(n, N_GROUPS_E, EXPERTS_PER_GROUP)
    e_in = jnp.take_along_axis(e_logits, g_idx[:, :, None], axis=1)[:, 0]
    w_top, e_idx = lax.top_k(jax.nn.softmax(e_in, axis=-1), TOP_K_E)
    w_top = w_top / jnp.sum(w_top, axis=-1, keepdims=True)
    gate_k = p_top * w_top
    expert_id = g_idx * EXPERTS_PER_GROUP + e_idx
    gates = jnp.sum(jax.nn.one_hot(expert_id, N_EXPERTS, dtype=jnp.float32) * gate_k[..., None], axis=1).astype(h.dtype)
    hid = jax.nn.silu(jnp.einsum('nd,edf->nef', t, w1)) * jnp.einsum('nd,edf->nef', t, w3)
    out = jnp.einsum('nef,efd->nd', hid * gates[:, :, None], w2)
    return out.reshape(bsz, L, d)


def setup_inputs(seed: int = 0) -> dict:
    key = jax.random.key(seed)
    ks = iter(jax.random.split(key, 64))
    f32 = jnp.float32

    def nrm(shape, scale):
        return jax.random.normal(next(ks), shape, f32) * scale

    def gain(shape):
        return 1.0 + nrm(shape, 0.02)

    def dt_bias(shape):
        dt = jnp.exp(jax.random.uniform(next(ks), shape, f32, np.log(1e-3), np.log(1e-1)))
        return dt + jnp.log(-jnp.expm1(-dt))

    inp = {}
    inp["x_prompt"] = nrm((BATCH, SEQ, D_MODEL), 1.0)
    inp["x_sample"] = nrm((DEC_BATCH, DEC_SEQ, D_MODEL), 1.0)
    inp["cache_a_k"] = nrm((DEC_BATCH, N_ATT, PAST_LEN, KVH_A, HD_A), 1.0)
    inp["cache_a_v"] = nrm((DEC_BATCH, N_ATT, PAST_LEN, KVH_A, HD_A), 1.0)
    inp["cache_b_ckv"] = nrm((DEC_BATCH, N_ATT, PAST_LEN, KV_RANK), 1.0)
    inp["cache_b_krope"] = nrm((DEC_BATCH, N_ATT, PAST_LEN, ROPE_B), 1.0)
    inp["state_c_fwd"] = nrm((DEC_BATCH, N_SSD, H_C, P_C, N_C), 0.1)
    inp["state_c_bwd"] = nrm((DEC_BATCH, N_SSD, H_C, P_C, N_C), 0.1)
    inp["c"] = nrm((DEC_BATCH, D_MODEL), 1.0)
    inp["c_ctx"] = nrm((D_MODEL,), 1.0)
    inp["ada_w"] = nrm((DEPTH, D_MODEL, 6 * D_MODEL), 0.5 * D_MODEL ** -0.5)
    inp["ada_b"] = nrm((DEPTH, 6 * D_MODEL), 0.02)
    inp["norm_mix"] = gain((DEPTH, D_MODEL))
    inp["norm_ffn"] = gain((DEPTH, D_MODEL))
    inp["att_w_in"] = nrm((N_ATT, D_MODEL, IN_ATT), D_MODEL ** -0.5)
    inp["att_a_qnorm"] = gain((N_ATT, HD_A))
    inp["att_a_knorm"] = gain((N_ATT, HD_A))
    inp["att_a_sink"] = nrm((N_ATT, H_A), 0.5)
    inp["att_b_qa_norm"] = gain((N_ATT, Q_RANK))
    inp["att_b_wuq"] = nrm((N_ATT, Q_RANK, H_B * QK_B), Q_RANK ** -0.5)
    inp["att_b_kva_norm"] = gain((N_ATT, KV_RANK))
    inp["att_b_wukv"] = nrm((N_ATT, KV_RANK, H_B * (NOPE_B + V_B)), KV_RANK ** -0.5)
    inp["att_b_qnorm"] = gain((N_ATT, QK_B))
    inp["att_b_knorm"] = gain((N_ATT, QK_B))
    inp["att_w_out"] = nrm((N_ATT, OUT_ATT, D_MODEL), OUT_ATT ** -0.5)
    inp["ssd_w_in"] = nrm((N_SSD, D_MODEL, IN_SSD), D_MODEL ** -0.5)
    inp["ssd_conv_w"] = nrm((N_SSD, D_CONV, CONV_CH), D_CONV ** -0.5)
    inp["ssd_conv_b"] = nrm((N_SSD, CONV_CH), 0.02)
    inp["ssd_alog_fwd"] = jnp.log(jax.random.uniform(next(ks), (N_SSD, H_C), f32, 1.0, 16.0))
    inp["ssd_alog_bwd"] = jnp.log(jax.random.uniform(next(ks), (N_SSD, H_C), f32, 1.0, 16.0))
    inp["ssd_dtb_fwd"] = dt_bias((N_SSD, H_C))
    inp["ssd_dtb_bwd"] = dt_bias((N_SSD, H_C))
    inp["ssd_d"] = 1.0 + nrm((N_SSD, H_C), 0.1)
    inp["ssd_norm"] = gain((N_SSD, D_INNER))
    inp["ssd_w_out"] = nrm((N_SSD, D_INNER, D_MODEL), D_INNER ** -0.5)
    inp["moe_wg"] = nrm((DEPTH, D_MODEL, N_GROUPS_E), D_MODEL ** -0.5)
    inp["moe_bg"] = nrm((DEPTH, N_GROUPS_E), 0.01)
    inp["moe_we"] = nrm((DEPTH, D_MODEL, N_EXPERTS), D_MODEL ** -0.5)
    inp["moe_be"] = nrm((DEPTH, N_EXPERTS), 0.01)
    inp["moe_w1"] = nrm((DEPTH, N_EXPERTS, D_MODEL, D_EXPERT), D_MODEL ** -0.5)
    inp["moe_w3"] = nrm((DEPTH, N_EXPERTS, D_MODEL, D_EXPERT), D_MODEL ** -0.5)
    inp["moe_w2"] = nrm((DEPTH, N_EXPERTS, D_EXPERT, D_MODEL), D_EXPERT ** -0.5)
    return inp


def reference(x_prompt, x_sample, cache_a_k, cache_a_v, cache_b_ckv, cache_b_krope, state_c_fwd, state_c_bwd,
              c, c_ctx, ada_w, ada_b, norm_mix, norm_ffn,
              att_w_in, att_a_qnorm, att_a_knorm, att_a_sink, att_b_qa_norm, att_b_wuq, att_b_kva_norm,
              att_b_wukv, att_b_qnorm, att_b_knorm, att_w_out,
              ssd_w_in, ssd_conv_w, ssd_conv_b, ssd_alog_fwd, ssd_alog_bwd, ssd_dtb_fwd, ssd_dtb_bwd, ssd_d,
              ssd_norm, ssd_w_out,
              moe_wg, moe_bg, moe_we, moe_be, moe_w1, moe_w3, moe_w2):
    yp = x_prompt
    ys = x_sample
    new_ak, new_av, new_ckv, new_kr, new_sf, new_sb = [], [], [], [], [], []
    for layer in range(DEPTH):
        sh1_p, sc1_p, g1_p, sh2_p, sc2_p, g2_p = adaln(c_ctx[None, :], ada_w[layer], ada_b[layer])
        sh1_s, sc1_s, g1_s, sh2_s, sc2_s, g2_s = adaln(c, ada_w[layer], ada_b[layer])
        hp = modulate(yp, norm_mix[layer], sh1_p, sc1_p)
        hs = modulate(ys, norm_mix[layer], sh1_s, sc1_s)
        if layer % 2 == 0:
            i = layer // 2
            ap = (att_w_in[i], att_a_qnorm[i], att_a_knorm[i], att_a_sink[i], att_b_qa_norm[i], att_b_wuq[i],
                  att_b_kva_norm[i], att_b_wukv[i], att_b_qnorm[i], att_b_knorm[i], att_w_out[i])
            op, ka, va, ckv, kr = att_mixer_context(hp, *ap)
            os_ = att_mixer_latent(hs, cache_a_k[:, i], cache_a_v[:, i], cache_b_ckv[:, i], cache_b_krope[:, i], *ap)
            new_ak.append(ka)
            new_av.append(va)
            new_ckv.append(ckv)
            new_kr.append(kr)
        else:
            j = layer // 2
            sp = (ssd_w_in[j], ssd_conv_w[j], ssd_conv_b[j], ssd_alog_fwd[j], ssd_alog_bwd[j], ssd_dtb_fwd[j],
                  ssd_dtb_bwd[j], ssd_d[j], ssd_norm[j], ssd_w_out[j])
            zeros = jnp.zeros((yp.shape[0], H_C, P_C, N_C), yp.dtype)
            op, sf, sb = ssd_mixer(hp, zeros, zeros, *sp)
            os_, _, _ = ssd_mixer(hs, state_c_fwd[:, j], state_c_bwd[:, j], *sp)
            new_sf.append(sf)
            new_sb.append(sb)
        yp = yp + g1_p * op
        ys = ys + g1_s * os_
        mp = (moe_wg[layer], moe_bg[layer], moe_we[layer], moe_be[layer], moe_w1[layer], moe_w3[layer], moe_w2[layer])
        yp = yp + g2_p * hier_moe(modulate(yp, norm_ffn[layer], sh2_p, sc2_p), *mp)
        ys = ys + g2_s * hier_moe(modulate(ys, norm_ffn[layer], sh2_s, sc2_s), *mp)
    new_a_k = jnp.stack(new_ak, axis=1)
    new_a_v = jnp.stack(new_av, axis=1)
    new_b_ckv = jnp.stack(new_ckv, axis=1)
    new_b_krope = jnp.stack(new_kr, axis=1)
    new_state_fwd = jnp.stack(new_sf, axis=1)
    new_state_bwd = jnp.stack(new_sb, axis=1)
    return (yp, ys, new_a_k, new_a_v, new_b_ckv, new_b_krope, new_state_fwd, new_state_bwd)
```

```python
import functools

import numpy as np
import jax
import jax.numpy as jnp
from jax import lax
from jax.experimental import pallas as pl
from jax.experimental.pallas import tpu as pltpu

F32 = jnp.float32
BF16 = jnp.bfloat16
EPS = 1e-6
GRID_W = 64
ROPE_THETA = 10000.0

D = 1024
H_A, KVH_A, HD_A = 8, 2, 64
WINDOW = 128
H_B, Q_RANK, KV_RANK, NOPE_B, ROPE_B, V_B = 8, 256, 128, 64, 32, 64
QK_B = NOPE_B + ROPE_B
HP_B = 128
D_INNER, P_C, G_C, N_C, D_CONV, CHUNK = 2048, 64, 8, 128, 5, 128
H_C = D_INNER // P_C
R_C = H_C // G_C
CONV_CH = D_INNER + 2 * G_C * N_C
N_GROUPS_E, EPG, N_EXPERTS, D_EXPERT = 4, 8, 32, 256

TM = 256
HALO = 8
TM_MOE = 1024
LANE = 128
VMEM_LIMIT = 56 << 20


def _cparams(*sem):
    return pltpu.CompilerParams(dimension_semantics=sem, vmem_limit_bytes=VMEM_LIMIT)


def _mm(a, b):
    return jnp.dot(a, b, preferred_element_type=F32)


def _mm_nt(a, b):
    return lax.dot_general(a, b, (((1,), (1,)), ((), ())), preferred_element_type=F32)


def _mm_tn(a, b):
    return lax.dot_general(a, b, (((0,), (0,)), ((), ())), preferred_element_type=F32)


def _split2(a):
    hi = a.astype(BF16)
    lo = (a - hi.astype(F32)).astype(BF16)
    return hi, lo


def _split3(a):
    hi = a.astype(BF16)
    r = a - hi.astype(F32)
    mid = r.astype(BF16)
    lo = (r - mid.astype(F32)).astype(BF16)
    return hi, mid, lo


def _mm_x2(a, b_exact):
    hi, lo = _split2(a)
    return _mm(hi, b_exact) + _mm(lo, b_exact)


def _mm_x3(a, b_exact):
    hi, mid, lo = _split3(a)
    return _mm(hi, b_exact) + _mm(mid, b_exact) + _mm(lo, b_exact)


def _mm_3x(a_exact, b):
    hi, mid, lo = _split3(b)
    return _mm(a_exact, hi) + _mm(a_exact, mid) + _mm(a_exact, lo)


def _sigmoid(x):
    return 1.0 / (1.0 + jnp.exp(-x))


def _silu(x):
    return x * _sigmoid(x)


def _rms(x):
    return x * lax.rsqrt(jnp.mean(x * x, axis=-1, keepdims=True) + EPS)


def _modulate(x, g, shift, scale):
    return _rms(x) * g * (1.0 + scale) + shift


def _headnorm(x, ind, ind_t, width):
    ssq = _mm_x2(x * x, ind)
    rs = lax.rsqrt(ssq * (1.0 / width) + EPS)
    return x * _mm_x2(rs, ind_t)


def _rope(x, cos, sin_n, sin_p, shift):
    n = x.shape[-1]
    return x * cos + pltpu.roll(x, n - shift, 1) * sin_n + pltpu.roll(x, shift, 1) * sin_p


def _tile_lanes(t, reps):
    return t if reps == 1 else jnp.concatenate([t] * reps, axis=1)


def _softmax_attend(q, pieces, sink=None):
    scores = []
    for k, _, mask in pieces:
        s = _mm_nt(q, k)
        scores.append(s if mask is None else s + mask)
    m = functools.reduce(jnp.maximum, [jnp.max(s, axis=-1, keepdims=True) for s in scores])
    if sink is not None:
        m = jnp.maximum(m, sink)
    denom = None
    out = None
    for s, (_, v, _) in zip(scores, pieces):
        p = jnp.exp(s - m)
        d = jnp.sum(p, axis=-1, keepdims=True)
        o = _mm(p.astype(BF16), v)
        denom = d if denom is None else denom + d
        out = o if out is None else out + o
    if sink is not None:
        denom = denom + jnp.exp(sink - m)
    return out * (1.0 / denom)


def _ada_kernel(c_ref, w_ref, b_ref, o_ref):
    c = c_ref[...]
    o_ref[...] = jnp.dot(_silu(c), w_ref[...], precision=lax.Precision.HIGHEST,
                         preferred_element_type=F32) + b_ref[...]


def _adaln(cond8, ada_w, ada_b):
    depth, d, n = ada_w.shape
    tn = 1536
    return pl.pallas_call(
        _ada_kernel,
        grid=(depth, n // tn),
        in_specs=[pl.BlockSpec((8, d), lambda l, j: (0, 0)),
                  pl.BlockSpec((None, d, tn), lambda l, j: (l, 0, j)),
                  pl.BlockSpec((None, 1, tn), lambda l, j: (l, 0, j))],
        out_specs=pl.BlockSpec((None, 8, tn), lambda l, j: (l, 0, j)),
        out_shape=jax.ShapeDtypeStruct((depth, 8, n), F32),
        compiler_params=_cparams("arbitrary", "arbitrary"),
        name="adaln",
    )(cond8, ada_w, ada_b.reshape(depth, 1, n))


def _mod_spec(col, row_of_tile):
    return pl.BlockSpec((None, 1, D), lambda i, *_: (row_of_tile(i), 0, col))


def _mla_keys(ckv_n, kr, wk, wv, e_rope, ind_b, ind_bt, gk_b):
    c16 = ckv_n.astype(BF16)
    kb = _mm(c16, wk) + _mm_x2(kr, e_rope)
    kb = _headnorm(kb, ind_b, ind_bt, QK_B) * gk_b
    vb = _mm(c16, wv)
    return kb, vb


def _att_proj_kernel(*refs, use_rope):
    (x_ref, shift_ref, scale_ref, g_ref, w_ref, gqa_ref, gka_ref, gql_ref, gkv_ref, gqb_ref, gkb_ref,
     wuq_ref, wk_ref, wv_ref, erope_ref, inda_ref, indat_ref, indk_ref, indkt_ref, indb_ref, indbt_ref) = refs[:21]
    refs = refs[21:]
    if use_rope:
        ca_ref, sna_ref, spa_ref, cb_ref, snb_ref, spb_ref = refs[:6]
        refs = refs[6:]
    (qa_o, ka_o, va_o, qb_o, kb_o, vb_o, ka32_o, va32_o, ckv32_o, kr32_o) = refs

    h = _modulate(x_ref[...], g_ref[...], shift_ref[...], scale_ref[...]).astype(BF16)
    proj = _mm(h, w_ref[...])
    o = 0
    qa = proj[:, o:o + H_A * HD_A]; o += H_A * HD_A
    ka = proj[:, o:o + KVH_A * HD_A]; o += KVH_A * HD_A
    va = proj[:, o:o + KVH_A * HD_A]; o += KVH_A * HD_A
    ql = proj[:, o:o + Q_RANK]; o += Q_RANK
    ckv = proj[:, o:o + KV_RANK]; o += KV_RANK
    kr = proj[:, o:o + ROPE_B]

    qa = _headnorm(qa, inda_ref[...], indat_ref[...], HD_A) * gqa_ref[...]
    ka = _headnorm(ka, indk_ref[...], indkt_ref[...], HD_A) * gka_ref[...]
    qb = _mm((_rms(ql) * gql_ref[...]).astype(BF16), wuq_ref[...])
    qb = _headnorm(qb, indb_ref[...], indbt_ref[...], QK_B) * gqb_ref[...]
    ckv_n = _rms(ckv) * gkv_ref[...]
    kb, vb = _mla_keys(ckv_n, kr, wk_ref[...], wv_ref[...], erope_ref[...], indb_ref[...], indbt_ref[...],
                       gkb_ref[...])
    ka32_o[...] = ka
    va32_o[...] = va
    ckv32_o[...] = ckv_n
    kr32_o[...] = kr
    if use_rope:
        ca, sna, spa = ca_ref[...], sna_ref[...], spa_ref[...]
        qa = _rope(qa, _tile_lanes(ca, 4), _tile_lanes(sna, 4), _tile_lanes(spa, 4), HD_A // 4)
        ka = _rope(ka, ca, sna, spa, HD_A // 4)
        cb, snb, spb = (_tile_lanes(t[...], H_B) for t in (cb_ref, snb_ref, spb_ref))
        qb = _rope(qb, cb, snb, spb, ROPE_B // 4)
        kb = _rope(kb, cb, snb, spb, ROPE_B // 4)
    qa_o[...] = (qa * HD_A ** -0.5).astype(BF16)
    ka_o[...] = ka.astype(BF16)
    va_o[...] = va.astype(BF16)
    qb_o[...] = (qb * QK_B ** -0.5).astype(BF16)
    kb_o[...] = kb.astype(BF16)
    vb_o[...] = vb.astype(BF16)


def _mla_ctx_kernel(ckv_ref, kr_ref, wk_ref, wv_ref, erope_ref, indb_ref, indbt_ref, gkb_ref, kb_o, vb_o):
    kb, vb = _mla_keys(ckv_ref[...], kr_ref[...], wk_ref[...], wv_ref[...], erope_ref[...], indb_ref[...],
                       indbt_ref[...], gkb_ref[...])
    kb_o[...] = kb.astype(BF16)
    vb_o[...] = vb.astype(BF16)


def _full(a):
    nd = a.ndim
    return pl.BlockSpec(a.shape, lambda *_: (0,) * nd)


def _rows(width, tm=TM):
    return pl.BlockSpec((tm, width), lambda i, *_: (i, 0))


def _att_project(x, mod, row_of_tile, g_mix, aw, rope_tabs, tiles_per_seq):
    t = x.shape[0]
    use_rope = rope_tabs is not None
    consts = [aw[k] for k in ("w_in", "gqa", "gka", "gql", "gkv", "gqb", "gkb", "wuq", "wk", "wv", "erope",
                              "inda", "indat", "indk", "indkt", "indb", "indbt")]
    ins = [x, mod, mod, g_mix] + consts
    specs = [_rows(D), _mod_spec(0, row_of_tile), _mod_spec(1, row_of_tile), _full(g_mix)] + [_full(c) for c in consts]
    if use_rope:
        ins += list(rope_tabs)
        specs += [pl.BlockSpec((TM, LANE), lambda i: (i % tiles_per_seq, 0))] * 6
    widths = [(H_A * HD_A, BF16), (KVH_A * HD_A, BF16), (KVH_A * HD_A, BF16), (H_B * HP_B, BF16),
              (H_B * HP_B, BF16), (H_B * V_B, BF16), (KVH_A * HD_A, F32), (KVH_A * HD_A, F32), (KV_RANK, F32),
              (ROPE_B, F32)]
    return pl.pallas_call(
        functools.partial(_att_proj_kernel, use_rope=use_rope),
        grid=(t // TM,),
        in_specs=specs,
        out_specs=[_rows(w) for w, _ in widths],
        out_shape=[jax.ShapeDtypeStruct((t, w), dt) for w, dt in widths],
        compiler_params=_cparams("arbitrary"),
        name="att_proj",
    )(*ins)


def _mla_ctx_keys(ckv, kr, aw):
    t = ckv.shape[0]
    consts = [aw[k] for k in ("wk", "wv", "erope", "indb", "indbt", "gkb")]
    return pl.pallas_call(
        _mla_ctx_kernel,
        grid=(t // TM,),
        in_specs=[_rows(KV_RANK), _rows(ROPE_B)] + [_full(c) for c in consts],
        out_specs=[_rows(H_B * HP_B), _rows(H_B * V_B)],
        out_shape=[jax.ShapeDtypeStruct((t, H_B * HP_B), BF16), jax.ShapeDtypeStruct((t, H_B * V_B), BF16)],
        compiler_params=_cparams("arbitrary"),
        name="mla_ctx_keys",
    )(ckv, kr, *consts)


def _out_residual(x_ref, gate_ref, o_scr, w_ref, y_ref):
    y_ref[...] = x_ref[...] + gate_ref[...] * _mm(o_scr[...], w_ref[...])


def _att_ctx_kernel(sink_ref, x_ref, gate_ref, qa_ref, ka_ref, va_ref, qb_ref, kb_ref, vb_ref, w_ref, y_ref, o_scr):
    grp = H_A // KVH_A
    for hh in range(H_A):
        kv = hh // grp
        o = _softmax_attend(qa_ref[:, hh * HD_A:(hh + 1) * HD_A],
                            [(ka_ref[:, kv * HD_A:(kv + 1) * HD_A], va_ref[:, kv * HD_A:(kv + 1) * HD_A], None)],
                            sink=sink_ref[hh])
        o_scr[:, hh * HD_A:(hh + 1) * HD_A] = o.astype(BF16)
    for hh in range(H_B):
        o = _softmax_attend(qb_ref[:, hh * HP_B:(hh + 1) * HP_B],
                            [(kb_ref[:, hh * HP_B:(hh + 1) * HP_B], vb_ref[:, hh * V_B:(hh + 1) * V_B], None)])
        o_scr[:, H_A * HD_A + hh * V_B:H_A * HD_A + (hh + 1) * V_B] = o.astype(BF16)
    _out_residual(x_ref, gate_ref, o_scr, w_ref, y_ref)


def _att_context(x, mod, sink, qa, ka, va, qb, kb, vb, w_out, seq):
    t = x.shape[0]
    assert seq == TM
    return pl.pallas_call(
        _att_ctx_kernel,
        grid=(t // TM,),
        in_specs=[pl.BlockSpec(memory_space=pltpu.SMEM), _rows(D), _mod_spec(2, lambda i: 0),
                  _rows(H_A * HD_A), _rows(KVH_A * HD_A), _rows(KVH_A * HD_A),
                  _rows(H_B * HP_B), _rows(H_B * HP_B), _rows(H_B * V_B), _full(w_out)],
        out_specs=_rows(D),
        out_shape=jax.ShapeDtypeStruct((t, D), F32),
        scratch_shapes=[pltpu.VMEM((TM, D), BF16)],
        compiler_params=_cparams("arbitrary"),
        name="att_context",
    )(sink, x, mod, qa, ka, va, qb, kb, vb, w_out)


def _att_lat_kernel(sink_ref, x_ref, gate_ref, qa_ref, qb_ref, ka_ref, va_ref, kb_ref, vb_ref,
                    cka_ref, cva_ref, ckb_ref, cvb_ref, w_ref, y_ref, o_scr, *, seq):
    j = pl.program_id(1)
    band = TM + 2 * WINDOW
    start = j * TM
    kstart = pl.multiple_of(jnp.clip(start - WINDOW, 0, seq - band), WINDOW)
    qpos = start + lax.broadcasted_iota(jnp.int32, (TM, band), 0)
    kpos = kstart + lax.broadcasted_iota(jnp.int32, (TM, band), 1)
    mask = jnp.where(jnp.abs(qpos - kpos) <= WINDOW, 0.0, -jnp.inf).astype(F32)
    ka_band = ka_ref[pl.ds(kstart, band), :]
    va_band = va_ref[pl.ds(kstart, band), :]
    grp = H_A // KVH_A
    for hh in range(H_A):
        kv = hh // grp
        sl = slice(kv * HD_A, (kv + 1) * HD_A)
        o = _softmax_attend(qa_ref[:, hh * HD_A:(hh + 1) * HD_A],
                            [(cka_ref[:, sl], cva_ref[:, sl], None), (ka_band[:, sl], va_band[:, sl], mask)],
                            sink=sink_ref[hh])
        o_scr[:, hh * HD_A:(hh + 1) * HD_A] = o.astype(BF16)
    for hh in range(H_B):
        ks = slice(hh * HP_B, (hh + 1) * HP_B)
        vs = slice(hh * V_B, (hh + 1) * V_B)
        o = _softmax_attend(qb_ref[:, ks], [(ckb_ref[:, ks], cvb_ref[:, vs], None),
                                            (kb_ref[:, ks], vb_ref[:, vs], None)])
        o_scr[:, H_A * HD_A + hh * V_B:H_A * HD_A + (hh + 1) * V_B] = o.astype(BF16)
    _out_residual(x_ref, gate_ref, o_scr, w_ref, y_ref)


def _att_latent(x, mod, sink, qa, ka, va, qb, kb, vb, cka, cva, ckb, cvb, w_out, seq, past):
    t = x.shape[0]
    nb = t // seq
    nq = seq // TM
    qrow = lambda w: pl.BlockSpec((TM, w), lambda b, j: (b * nq + j, 0))
    per_b = lambda rows, w: pl.BlockSpec((rows, w), lambda b, j: (b, 0))
    return pl.pallas_call(
        functools.partial(_att_lat_kernel, seq=seq),
        grid=(nb, nq),
        in_specs=[pl.BlockSpec(memory_space=pltpu.SMEM), qrow(D),
                  pl.BlockSpec((None, 1, D), lambda b, j: (1 + b, 0, 2)),
                  qrow(H_A * HD_A), qrow(H_B * HP_B),
                  per_b(seq, KVH_A * HD_A), per_b(seq, KVH_A * HD_A), per_b(seq, H_B * HP_B), per_b(seq, H_B * V_B),
                  per_b(past, KVH_A * HD_A), per_b(past, KVH_A * HD_A), per_b(past, H_B * HP_B),
                  per_b(past, H_B * V_B), pl.BlockSpec(w_out.shape, lambda b, j: (0, 0))],
        out_specs=qrow(D),
        out_shape=jax.ShapeDtypeStruct((t, D), F32),
        scratch_shapes=[pltpu.VMEM((TM, D), BF16)],
        compiler_params=_cparams("arbitrary", "arbitrary"),
        name="att_latent",
    )(sink, x, mod, qa, qb, ka, va, kb, vb, cka, cva, ckb, cvb, w_out)


def _route(logits):
    lane = lax.broadcasted_iota(jnp.int32, logits.shape, 1)
    lane_f = lane.astype(F32)
    is_grp = (lane >= N_EXPERTS) & (lane < N_EXPERTS + N_GROUPS_E)
    lg = jnp.where(is_grp, logits, -jnp.inf)
    gmax = jnp.max(lg, axis=-1, keepdims=True)
    gsum = jnp.sum(jnp.where(is_grp, jnp.exp(lg - gmax), 0.0), axis=-1, keepdims=True)
    p_top = 1.0 / gsum
    gidx = jnp.min(jnp.where(lg == gmax, lane_f - N_EXPERTS, 1e9), axis=-1, keepdims=True)
    in_grp = (lane < N_EXPERTS) & (jnp.floor(lane_f * (1.0 / EPG)) == gidx)
    le = jnp.where(in_grp, logits, -jnp.inf)
    m1 = jnp.max(le, axis=-1, keepdims=True)
    i1 = jnp.min(jnp.where(le == m1, lane_f, 1e9), axis=-1, keepdims=True)
    le2 = jnp.where(lane_f == i1, -jnp.inf, le)
    m2 = jnp.max(le2, axis=-1, keepdims=True)
    i2 = jnp.min(jnp.where(le2 == m2, lane_f, 1e9), axis=-1, keepdims=True)
    r = jnp.exp(m2 - m1)
    w1 = 1.0 / (1.0 + r)
    return jnp.where(lane_f == i1, p_top * w1, 0.0) + jnp.where(lane_f == i2, p_top * (r * w1), 0.0)


def _moe_dense_kernel(x_ref, shift_ref, scale_ref, gate_ref, g_ref, wr_ref, br_ref, w1_ref, w3_ref, w2_ref,
                      y_ref, t_scr, gates_scr, acc_scr):
    e = pl.program_id(1)

    @pl.when(e == 0)
    def _():
        t = _modulate(x_ref[...], g_ref[...], shift_ref[...], scale_ref[...])
        t_scr[...] = t.astype(BF16)
        logits = jnp.dot(t, wr_ref[...], precision=lax.Precision.HIGHEST, preferred_element_type=F32) + br_ref[...]
        gates_scr[...] = _route(logits)
        acc_scr[...] = jnp.zeros_like(acc_scr)

    t = t_scr[...]
    lane = lax.broadcasted_iota(jnp.int32, gates_scr.shape, 1)
    ge = jnp.sum(jnp.where(lane == e, gates_scr[...], 0.0), axis=-1, keepdims=True)
    h1 = _mm(t, w1_ref[...].astype(BF16))
    h3 = _mm(t, w3_ref[...].astype(BF16))
    hid = (_silu(h1) * h3 * ge).astype(BF16)
    acc_scr[...] += _mm(hid, w2_ref[...].astype(BF16))

    @pl.when(e == pl.num_programs(1) - 1)
    def _():
        y_ref[...] = x_ref[...] + gate_ref[...] * acc_scr[...]


def _moe(x, mod, row_of_tile, g_ffn, w_route, b_route, w1, w3, w2):
    t = x.shape[0]
    tm = TM_MOE
    row = lambda i, e: (i, 0)
    mspec = lambda col: pl.BlockSpec((None, 1, D), lambda i, e: (row_of_tile(i, tm), 0, col))
    return pl.pallas_call(
        _moe_dense_kernel,
        grid=(t // tm, N_EXPERTS),
        in_specs=[pl.BlockSpec((tm, D), row), mspec(3), mspec(4), mspec(5),
                  pl.BlockSpec(g_ffn.shape, lambda i, e: (0, 0)),
                  pl.BlockSpec(w_route.shape, lambda i, e: (0, 0)),
                  pl.BlockSpec(b_route.shape, lambda i, e: (0, 0)),
                  pl.BlockSpec((None, D, D_EXPERT), lambda i, e: (e, 0, 0)),
                  pl.BlockSpec((None, D, D_EXPERT), lambda i, e: (e, 0, 0)),
                  pl.BlockSpec((None, D_EXPERT, D), lambda i, e: (e, 0, 0))],
        out_specs=pl.BlockSpec((tm, D), row),
        out_shape=jax.ShapeDtypeStruct((t, D), F32),
        scratch_shapes=[pltpu.VMEM((tm, D), BF16), pltpu.VMEM((tm, LANE), F32), pltpu.VMEM((tm, D), F32)],
        compiler_params=_cparams("arbitrary", "arbitrary"),
        name="moe_dense",
    )(x, mod, mod, mod, g_ffn, w_route, b_route, w1, w3, w2)


def _ssd_proj_kernel(xm_ref, xp_ref, xn_ref, shift_ref, scale_ref, g_ref, wz_ref, wx_ref, wdt_ref, cw_ref, cb_ref,
                     dtb_ref, z_ref, xc_ref, dt_ref, xe_scr, *, tiles_per_seq):
    pos = pl.program_id(0) % tiles_per_seq
    first = pos == 0
    last = pos == tiles_per_seq - 1
    xe = jnp.concatenate([xp_ref[...], xm_ref[...], xn_ref[...]], axis=0)
    h = _modulate(xe, g_ref[...], shift_ref[...], scale_ref[...]).astype(BF16)
    hm = h[HALO:HALO + TM]
    z_ref[...] = _mm(hm, wz_ref[...]).astype(BF16)
    dt = _mm(hm, wdt_ref[...]) + dtb_ref[...]
    dt_ref[...] = jnp.maximum(dt, 0.0) + jnp.log(1.0 + jnp.exp(-jnp.abs(dt)))
    rid = lax.broadcasted_iota(jnp.int32, (TM + 2 * HALO, 1), 0)
    lo = jnp.where(first, HALO, 0)
    hi = jnp.where(last, HALO + TM, TM + 2 * HALO)
    valid = (rid >= lo) & (rid < hi)
    cw = cw_ref[...]
    nchunk = 1024
    for c in range(CONV_CH // nchunk):
        cs = slice(c * nchunk, (c + 1) * nchunk)
        xe_scr[...] = jnp.where(valid, _mm(h, wx_ref[:, cs]), 0.0)
        acc = cb_ref[:, cs]
        for k in range(D_CONV):
            acc = acc + cw[k:k + 1, cs] * xe_scr[pl.ds(HALO - D_CONV // 2 + k, TM), :]
        xc_ref[:, cs] = _silu(acc).astype(BF16)


def _ssd_project(x, mod, row_of_tile, g_mix, sw, tiles_per_seq):
    t = x.shape[0]
    r = TM // HALO
    nb8 = t // HALO
    consts = [sw[k] for k in ("wz", "wx", "wdt", "conv_w", "conv_b", "dtb")]
    return pl.pallas_call(
        functools.partial(_ssd_proj_kernel, tiles_per_seq=tiles_per_seq),
        grid=(t // TM,),
        in_specs=[_rows(D),
                  pl.BlockSpec((HALO, D), lambda i: (jnp.maximum(i * r - 1, 0), 0)),
                  pl.BlockSpec((HALO, D), lambda i: (jnp.minimum((i + 1) * r, nb8 - 1), 0)),
                  _mod_spec(0, row_of_tile), _mod_spec(1, row_of_tile), _full(g_mix)] + [_full(c) for c in consts],
        out_specs=[_rows(D_INNER), _rows(CONV_CH), _rows(LANE)],
        out_shape=[jax.ShapeDtypeStruct((t, D_INNER), BF16), jax.ShapeDtypeStruct((t, CONV_CH), BF16),
                   jax.ShapeDtypeStruct((t, LANE), F32)],
        scratch_shapes=[pltpu.VMEM((TM + 2 * HALO, 1024), F32)],
        compiler_params=_cparams("arbitrary"),
        name="ssd_proj",
    )(x, x, x, mod, mod, g_mix, *consts)


def _scan_direction(d, xc_ref, dtb, a_row, a_col, linc, uinc, expand, st_ref, y_ref):
    off = H_C * d
    dt_t = dtb.T
    da = dtb * a_row
    da_t = dt_t * a_col
    acs = _mm_3x(linc if d == 0 else uinc, da)
    acs_t = _mm_x3(da_t, uinc if d == 0 else linc)
    tot = acs[CHUNK - 1:CHUNK, :] if d == 0 else acs[0:1, :]
    w_all = jnp.exp(tot - acs) * dtb
    eacs = jnp.exp(acs)
    cdec = _mm_x3(jnp.broadcast_to(jnp.exp(tot), (8, LANE)), expand)[0:1, :]
    ri = lax.broadcasted_iota(jnp.int32, (CHUNK, CHUNK), 0)
    ci = lax.broadcasted_iota(jnp.int32, (CHUNK, CHUNK), 1)
    keep = (ri >= ci) if d == 0 else (ci >= ri)
    gw = R_C * P_C
    for g in range(G_C):
        bg = xc_ref[:, D_INNER + g * N_C:D_INNER + (g + 1) * N_C]
        cg = xc_ref[:, D_INNER + G_C * N_C + g * N_C:D_INNER + G_C * N_C + (g + 1) * N_C]
        cb = _mm_nt(cg, bg)
        sp = st_ref[:, g * gw:(g + 1) * gw]
        yint = _mm(cg, sp.astype(BF16))
        xw = []
        for r in range(R_C):
            hh = g * R_C + r
            ln = off + hh
            seg = acs[:, ln:ln + 1] - acs_t[ln:ln + 1, :]
            dec = jnp.exp(jnp.where(keep, seg, -jnp.inf))
            m = (cb * dec * dt_t[ln:ln + 1, :]).astype(BF16)
            xh = xc_ref[:, hh * P_C:(hh + 1) * P_C]
            yh = _mm(m, xh) + eacs[:, ln:ln + 1] * yint[:, r * P_C:(r + 1) * P_C]
            y_ref[:, hh * P_C:(hh + 1) * P_C] = yh.astype(BF16)
            xw.append((xh.astype(F32) * w_all[:, ln:ln + 1]).astype(BF16))
        sc = _mm_tn(bg, jnp.concatenate(xw, axis=1))
        st_ref[:, g * gw:(g + 1) * gw] = sp * cdec[:, g * gw:(g + 1) * gw] + sc


def _ssd_scan_kernel(*refs, has_s0, want_final):
    xf_ref, xb_ref, dtf_ref, dtbw_ref, arow_ref, acol_ref, linc_ref, uinc_ref, expf_ref, expb_ref = refs[:10]
    refs = refs[10:]
    if has_s0:
        s0f_ref, s0b_ref = refs[:2]
        refs = refs[2:]
    yf_ref, yb_ref = refs[:2]
    refs = refs[2:]
    if want_final:
        finf_ref, finb_ref = refs[:2]
        refs = refs[2:]
    sf_scr, sb_scr = refs
    s = pl.program_id(1)
    nblk = (H_C * P_C) // N_C

    @pl.when(s == 0)
    def _():
        if has_s0:
            for b in range(nblk):
                sf_scr[:, b * N_C:(b + 1) * N_C] = s0f_ref[b * N_C:(b + 1) * N_C, :].T
                sb_scr[:, b * N_C:(b + 1) * N_C] = s0b_ref[b * N_C:(b + 1) * N_C, :].T
        else:
            sf_scr[...] = jnp.zeros_like(sf_scr)
            sb_scr[...] = jnp.zeros_like(sb_scr)

    a_row = -jnp.exp(arow_ref[...])
    a_col = -jnp.exp(acol_ref[...])
    linc, uinc = linc_ref[...], uinc_ref[...]
    _scan_direction(0, xf_ref, dtf_ref[...], a_row, a_col, linc, uinc, expf_ref[...], sf_scr, yf_ref)
    _scan_direction(1, xb_ref, dtbw_ref[...], a_row, a_col, linc, uinc, expb_ref[...], sb_scr, yb_ref)

    if want_final:
        @pl.when(s == pl.num_programs(1) - 1)
        def _():
            for b in range(nblk):
                finf_ref[b * N_C:(b + 1) * N_C, :] = sf_scr[:, b * N_C:(b + 1) * N_C].T
                finb_ref[b * N_C:(b + 1) * N_C, :] = sb_scr[:, b * N_C:(b + 1) * N_C].T


def _ssd_scan(xc, dt, sw, seq, s0=None, want_final=False):
    t = xc.shape[0]
    nseq = t // seq
    nc = seq // CHUNK
    fwd = lambda w: pl.BlockSpec((CHUNK, w), lambda b, s: (b * nc + s, 0))
    bwd = lambda w: pl.BlockSpec((CHUNK, w), lambda b, s: (b * nc + nc - 1 - s, 0))
    consts = [sw[k] for k in ("alog_row", "alog_col", "linc", "uinc", "expand_f", "expand_b")]
    ins = [xc, xc, dt, dt] + consts
    specs = [fwd(CONV_CH), bwd(CONV_CH), fwd(LANE), bwd(LANE)] + [pl.BlockSpec(c.shape, lambda b, s: (0, 0))
                                                                    for c in consts]
    state = pl.BlockSpec((None, H_C * P_C, N_C), lambda b, s: (b, 0, 0))
    if s0 is not None:
        ins += list(s0)
        specs += [state, state]
    out_specs = [fwd(D_INNER), bwd(D_INNER)]
    out_shape = [jax.ShapeDtypeStruct((t, D_INNER), BF16)] * 2
    if want_final:
        out_specs += [state, state]
        out_shape += [jax.ShapeDtypeStruct((nseq, H_C * P_C, N_C), F32)] * 2
    return pl.pallas_call(
        functools.partial(_ssd_scan_kernel, has_s0=s0 is not None, want_final=want_final),
        grid=(nseq, nc),
        in_specs=specs,
        out_specs=out_specs,
        out_shape=out_shape,
        scratch_shapes=[pltpu.VMEM((N_C, H_C * P_C), F32)] * 2,
        compiler_params=_cparams("arbitrary", "arbitrary"),
        name="ssd_scan",
    )(*ins)


def _ssd_out_kernel(x_ref, gate_ref, z_ref, xs_ref, yf_ref, yb_ref, dskip_ref, nw_ref, w_ref, y_ref):
    y = yf_ref[...].astype(F32) + yb_ref[...].astype(F32) + dskip_ref[...] * xs_ref[...].astype(F32)
    y = y * _silu(z_ref[...].astype(F32))
    gw = D_INNER // G_C
    yn = jnp.concatenate([_rms(y[:, g * gw:(g + 1) * gw]) for g in range(G_C)], axis=1) * nw_ref[...]
    y_ref[...] = x_ref[...] + gate_ref[...] * _mm(yn.astype(BF16), w_ref[...])


def _ssd_output(x, mod, row_of_tile, z, xc, yf, yb, sw):
    t = x.shape[0]
    consts = [sw[k] for k in ("dskip", "norm_w", "w_out")]
    return pl.pallas_call(
        _ssd_out_kernel,
        grid=(t // TM,),
        in_specs=[_rows(D), _mod_spec(2, row_of_tile), _rows(D_INNER), _rows(D_INNER), _rows(D_INNER),
                  _rows(D_INNER)] + [_full(c) for c in consts],
        out_specs=_rows(D),
        out_shape=jax.ShapeDtypeStruct((t, D), F32),
        compiler_params=_cparams("arbitrary"),
        name="ssd_out",
    )(x, mod, z, xc, yf, yb, *consts)


def _indicator(width, per_head, heads_of_lane=None):
    ind = np.zeros((width, LANE), np.float32)
    ind[np.arange(width), np.arange(width) // per_head] = 1.0
    return jnp.asarray(ind, BF16), jnp.asarray(ind.T.copy(), BF16)


def _rope_tables(seq, rot_dim, lanes, at):
    t = np.arange(seq)
    quarter = rot_dim // 4
    inv = ROPE_THETA ** (-np.arange(quarter, dtype=np.float32) / quarter)
    ar = (t // GRID_W).astype(np.float32)[:, None] * inv[None, :]
    ac = (t % GRID_W).astype(np.float32)[:, None] * inv[None, :]
    ang = np.concatenate([ar, ar, ac, ac], axis=-1).astype(np.float32)
    cos, sin = np.cos(ang), np.sin(ang)
    first = (np.arange(rot_dim) % (rot_dim // 2)) < quarter
    c = np.ones((seq, lanes), np.float32)
    sn = np.zeros((seq, lanes), np.float32)
    sp = np.zeros((seq, lanes), np.float32)
    for base in range(0, lanes, max(rot_dim, lanes if at else rot_dim)):
        c[:, base + at:base + at + rot_dim] = cos
        sn[:, base + at:base + at + rot_dim] = np.where(first, -sin, 0.0)
        sp[:, base + at:base + at + rot_dim] = np.where(first, 0.0, sin)
    return jnp.asarray(c), jnp.asarray(sn), jnp.asarray(sp)


def _pad_heads(w, width, at=0):
    rows = w.shape[0]
    out = jnp.zeros((rows, H_B, HP_B), w.dtype)
    out = out.at[:, :, at:at + width].set(w.reshape(rows, H_B, width))
    return out.reshape(rows, H_B * HP_B)


def _att_weights(i, att_w_in, att_a_qnorm, att_a_knorm, att_b_qa_norm, att_b_wuq, att_b_kva_norm, att_b_wukv,
                 att_b_qnorm, att_b_knorm):
    n_in = att_w_in.shape[-1]
    aw = {}
    aw["w_in"] = jnp.pad(att_w_in[i], ((0, 0), (0, -n_in % LANE))).astype(BF16)
    aw["gqa"] = jnp.tile(att_a_qnorm[i], H_A)[None, :]
    aw["gka"] = jnp.tile(att_a_knorm[i], KVH_A)[None, :]
    aw["gql"] = att_b_qa_norm[i][None, :]
    aw["gkv"] = att_b_kva_norm[i][None, :]
    aw["gqb"] = _pad_heads(jnp.tile(att_b_qnorm[i], H_B)[None, :], QK_B)
    aw["gkb"] = _pad_heads(jnp.tile(att_b_knorm[i], H_B)[None, :], QK_B)
    aw["wuq"] = _pad_heads(att_b_wuq[i], QK_B).astype(BF16)
    wukv = att_b_wukv[i].reshape(KV_RANK, H_B, NOPE_B + V_B)
    aw["wk"] = _pad_heads(wukv[:, :, :NOPE_B].reshape(KV_RANK, H_B * NOPE_B), NOPE_B).astype(BF16)
    aw["wv"] = wukv[:, :, NOPE_B:].reshape(KV_RANK, H_B * V_B).astype(BF16)
    er = np.zeros((ROPE_B, H_B, HP_B), np.float32)
    for r in range(ROPE_B):
        er[r, :, NOPE_B + r] = 1.0
    aw["erope"] = jnp.asarray(er.reshape(ROPE_B, H_B * HP_B), BF16)
    aw["inda"], aw["indat"] = _indicator(H_A * HD_A, HD_A)
    aw["indk"], aw["indkt"] = _indicator(KVH_A * HD_A, HD_A)
    aw["indb"], aw["indbt"] = _indicator(H_B * HP_B, HP_B)
    return aw


def _ssd_weights(j, ssd_w_in, ssd_conv_w, ssd_conv_b, ssd_alog_fwd, ssd_alog_bwd, ssd_dtb_fwd, ssd_dtb_bwd, ssd_d,
                 ssd_norm, ssd_w_out):
    sw = {}
    w = ssd_w_in[j]
    sw["wz"] = w[:, :D_INNER].astype(BF16)
    sw["wx"] = w[:, D_INNER:D_INNER + CONV_CH].astype(BF16)
    sw["wdt"] = jnp.pad(w[:, D_INNER + CONV_CH:], ((0, 0), (0, LANE - 2 * H_C))).astype(BF16)
    sw["conv_w"] = jnp.pad(ssd_conv_w[j], ((0, 8 - D_CONV), (0, 0)))
    sw["conv_b"] = ssd_conv_b[j][None, :]
    pad = LANE - 2 * H_C
    sw["dtb"] = jnp.pad(jnp.concatenate([ssd_dtb_fwd[j], ssd_dtb_bwd[j]]), (0, pad))[None, :]
    alog = jnp.pad(jnp.concatenate([ssd_alog_fwd[j], ssd_alog_bwd[j]]), (0, pad))
    sw["alog_row"] = alog[None, :]
    sw["alog_col"] = alog[:, None]
    tri = np.tril(np.ones((CHUNK, CHUNK), np.float32))
    sw["linc"] = jnp.asarray(tri, BF16)
    sw["uinc"] = jnp.asarray(tri.T.copy(), BF16)
    for name, off in (("expand_f", 0), ("expand_b", H_C)):
        ex = np.zeros((LANE, H_C * P_C), np.float32)
        for hh in range(H_C):
            ex[off + hh, hh * P_C:(hh + 1) * P_C] = 1.0
        sw[name] = jnp.asarray(ex, BF16)
    sw["dskip"] = jnp.repeat(ssd_d[j], P_C)[None, :]
    sw["norm_w"] = ssd_norm[j][None, :]
    sw["w_out"] = ssd_w_out[j].astype(BF16)
    return sw


def kernel(x_prompt, x_sample, cache_a_k, cache_a_v, cache_b_ckv, cache_b_krope, state_c_fwd, state_c_bwd, c, c_ctx, ada_w, ada_b, norm_mix, norm_ffn, att_w_in, att_a_qnorm, att_a_knorm, att_a_sink, att_b_qa_norm, att_b_wuq, att_b_kva_norm, att_b_wukv, att_b_qnorm, att_b_knorm, att_w_out, ssd_w_in, ssd_conv_w, ssd_conv_b, ssd_alog_fwd, ssd_alog_bwd, ssd_dtb_fwd, ssd_dtb_bwd, ssd_d, ssd_norm, ssd_w_out, moe_wg, moe_bg, moe_we, moe_be, moe_w1, moe_w3, moe_w2):
    bp, lp, _ = x_prompt.shape
    bs, ls, _ = x_sample.shape
    past = cache_a_k.shape[2]
    depth = ada_w.shape[0]
    assert bs + 1 <= 8 and lp == TM and ls % TM == 0

    cond8 = jnp.zeros((8, D), F32).at[0].set(c_ctx).at[1:1 + bs].set(c)
    mods = _adaln(cond8, ada_w, ada_b)
    yp = x_prompt.reshape(bp * lp, D)
    ys = x_sample.reshape(bs * ls, D)
    prow = lambda i, tm=TM: 0
    srow = lambda i, tm=TM: 1 + (i * tm) // ls
    rope_a = _rope_tables(ls, HD_A, LANE, 0)
    rope_b = _rope_tables(ls, ROPE_B, HP_B, NOPE_B)

    outs = {}
    for layer in range(depth):
        mod = mods[layer].reshape(8, 1, 6 * D)
        g_mix = norm_mix[layer][None, :]
        g_ffn = norm_ffn[layer][None, :]
        if layer % 2 == 0:
            i = layer // 2
            aw = _att_weights(i, att_w_in, att_a_qnorm, att_a_knorm, att_b_qa_norm, att_b_wuq, att_b_kva_norm,
                              att_b_wukv, att_b_qnorm, att_b_knorm)
            w_out = att_w_out[i].astype(BF16)
            sink = att_a_sink[i]
            qa, ka, va, qb, kb, vb, ka32, va32, ckv32, kr32 = _att_project(yp, mod, prow, g_mix, aw, None, 1)
            outs.setdefault("ak", []).append(ka32.reshape(bp, lp, KVH_A, HD_A))
            outs.setdefault("av", []).append(va32.reshape(bp, lp, KVH_A, HD_A))
            outs.setdefault("ckv", []).append(ckv32.reshape(bp, lp, KV_RANK))
            outs.setdefault("kr", []).append(kr32.reshape(bp, lp, ROPE_B))
            yp = _att_context(yp, mod, sink, qa, ka, va, qb, kb, vb, w_out, lp)
            qa, ka, va, qb, kb, vb = _att_project(ys, mod, srow, g_mix, aw, rope_a + rope_b, ls // TM)[:6]
            cka = cache_a_k[:, i].reshape(bs * past, KVH_A * HD_A).astype(BF16)
            cva = cache_a_v[:, i].reshape(bs * past, KVH_A * HD_A).astype(BF16)
            ckb, cvb = _mla_ctx_keys(cache_b_ckv[:, i].reshape(bs * past, KV_RANK),
                                     cache_b_krope[:, i].reshape(bs * past, ROPE_B), aw)
            ys = _att_latent(ys, mod, sink, qa, ka, va, qb, kb, vb, cka, cva, ckb, cvb, w_out, ls, past)
        else:
            j = layer // 2
            sw = _ssd_weights(j, ssd_w_in, ssd_conv_w, ssd_conv_b, ssd_alog_fwd, ssd_alog_bwd, ssd_dtb_fwd,
                              ssd_dtb_bwd, ssd_d, ssd_norm, ssd_w_out)
            z, xc, dt = _ssd_project(yp, mod, prow, g_mix, sw, lp // TM)
            yf, yb, fin_f, fin_b = _ssd_scan(xc, dt, sw, lp, want_final=True)
            outs.setdefault("sf", []).append(fin_f.reshape(bp, H_C, P_C, N_C))
            outs.setdefault("sb", []).append(fin_b.reshape(bp, H_C, P_C, N_C))
            yp = _ssd_output(yp, mod, prow, z, xc, yf, yb, sw)
            z, xc, dt = _ssd_project(ys, mod, srow, g_mix, sw, ls // TM)
            s0 = (state_c_fwd[:, j].reshape(bs, H_C * P_C, N_C), state_c_bwd[:, j].reshape(bs, H_C * P_C, N_C))
            yf, yb = _ssd_scan(xc, dt, sw, ls, s0=s0)
            ys = _ssd_output(ys, mod, srow, z, xc, yf, yb, sw)
        w_route = jnp.pad(jnp.concatenate([moe_we[layer], moe_wg[layer]], axis=1),
                          ((0, 0), (0, LANE - N_EXPERTS - N_GROUPS_E)))
        b_route = jnp.pad(jnp.concatenate([moe_be[layer], moe_bg[layer]]), (0, LANE - N_EXPERTS - N_GROUPS_E))[None, :]
        yp = _moe(yp, mod, prow, g_ffn, w_route, b_route, moe_w1[layer], moe_w3[layer], moe_w2[layer])
        ys = _moe(ys, mod, srow, g_ffn, w_route, b_route, moe_w1[layer], moe_w3[layer], moe_w2[layer])

    stack = lambda k: jnp.stack(outs[k], axis=1)
    return (yp.reshape(bp, lp, D), ys.reshape(bs, ls, D), stack("ak"), stack("av"), stack("ckv"), stack("kr"),
            stack("sf"), stack("sb"))
```
